```python
import math
import jax
import jax.numpy as jnp
from jax import lax
import numpy as np

D_MODEL = 2048
BATCH = 4
SEQ = 2048
DEPTH = 1
DEC_BATCH = 128
DEC_SEQ = 1
PAST_LEN = 8192
PAGE_SIZE = 128

N_HEADS = 16
N_KV_HEADS = 4
HEAD_DIM = 64
Q_PER_KV = N_HEADS // N_KV_HEADS
D_ATTN = N_HEADS * HEAD_DIM
D_KV = N_KV_HEADS * HEAD_DIM
WINDOW = 128
SSD_HEADS = 32
SSD_HEAD_DIM = 64
D_SSD = SSD_HEADS * SSD_HEAD_DIM
SSD_GROUPS = 4
HEADS_PER_GROUP = SSD_HEADS // SSD_GROUPS
D_STATE = 128
SSD_CONV = 4
SSD_CHUNK = 128
CONV_DIM = D_SSD + 2 * SSD_GROUPS * D_STATE
D_MIX = D_ATTN + D_SSD
IN_SPLITS = (D_ATTN, D_ATTN + D_KV, D_ATTN + 2 * D_KV, D_ATTN + 2 * D_KV + D_SSD,
             D_ATTN + 2 * D_KV + D_SSD + CONV_DIM)
D_IN = D_ATTN + 2 * D_KV + D_SSD + CONV_DIM + SSD_HEADS
N_MEM = 256
MEM_HEADS = 4
MEM_HEAD_DIM = 128
D_XATTN = MEM_HEADS * MEM_HEAD_DIM
D_FF = 11 * D_MODEL // 4
FFN_CONV = 3
EPS = 1e-6

kernel_name = 'hymba_swa_ssd_memxattn_convffn_step'


def rms_norm(x, g):
    xf = x.astype(jnp.float32)
    y = xf * lax.rsqrt(jnp.mean(xf * xf, axis=-1, keepdims=True) + EPS)
    return (y * g.astype(jnp.float32)).astype(x.dtype)


def causal_dwconv(u, past, w, b):
    k, t = w.shape[0], u.shape[1]
    full = jnp.concatenate([past.astype(u.dtype), u], axis=1)
    out = sum((full[:, i:i + t] * w[i] for i in range(k)), b)
    return out, full[:, t:]


def sink_softmax(logits, mask, sink):
    logits = jnp.where(mask, logits, -jnp.inf)
    sink = jnp.broadcast_to(sink.astype(jnp.float32), logits.shape[:-1] + (1,))
    return jax.nn.softmax(jnp.concatenate([logits, sink], axis=-1), axis=-1)[..., :-1]


def swa_banded(q, k, v, sinks):
    b, t = q.shape[:2]
    nb = t // WINDOW
    qb = q.reshape(b, nb, WINDOW, N_KV_HEADS, Q_PER_KV, HEAD_DIM)

    def band(a):
        a = a.reshape(b, nb, WINDOW, N_KV_HEADS, HEAD_DIM)
        prev = jnp.concatenate([jnp.zeros_like(a[:, :1]), a[:, :-1]], axis=1)
        return jnp.concatenate([prev, a], axis=2)

    kk, vv = band(k), band(v)
    logits = jnp.einsum('bnqkgd,bnskd->bnkgqs', qb, kk,
                        preferred_element_type=jnp.float32) * HEAD_DIM ** -0.5
    qi = jnp.arange(WINDOW)[:, None]
    sj = jnp.arange(2 * WINDOW)[None, :]
    diff = qi + WINDOW - sj
    blk = jnp.arange(nb)[:, None, None]
    mask = ((diff >= 0) & (diff <= WINDOW))[None] & ((blk > 0) | (sj >= WINDOW)[None])
    p = sink_softmax(logits, mask[None, :, None, None],
                     sinks.reshape(1, 1, N_KV_HEADS, Q_PER_KV, 1, 1))
    o = jnp.einsum('bnkgqs,bnskd->bnqkgd', p.astype(v.dtype), vv)
    return o.reshape(b, t, D_ATTN)


def swa_step(q, k, v, k_buf, v_buf, sinks):
    b, t = q.shape[:2]
    w = k_buf.shape[1]
    kk = jnp.concatenate([k_buf.astype(k.dtype), k], axis=1)
    vv = jnp.concatenate([v_buf.astype(v.dtype), v], axis=1)
    qg = q.reshape(b, t, N_KV_HEADS, Q_PER_KV, HEAD_DIM)
    logits = jnp.einsum('bqkgd,bskd->bkgqs', qg, kk,
                        preferred_element_type=jnp.float32) * HEAD_DIM ** -0.5
    diff = jnp.arange(t)[:, None] + w - jnp.arange(w + t)[None, :]
    mask = (diff >= 0) & (diff <= WINDOW)
    p = sink_softmax(logits, mask, sinks.reshape(1, N_KV_HEADS, Q_PER_KV, 1, 1))
    o = jnp.einsum('bkgqs,bskd->bqkgd', p.astype(v.dtype), vv).reshape(b, t, D_ATTN)
    return o, kk[:, t:], vv[:, t:]


def ssd_scan(xs, dt, a, bm, cm, h0, chunk):
    b, t = xs.shape[:2]
    nc = t // chunk
    xs = xs.reshape(b, nc, chunk, *xs.shape[2:])
    dt = dt.reshape(b, nc, chunk, *dt.shape[2:])
    bm = bm.reshape(b, nc, chunk, *bm.shape[2:])
    cm = cm.reshape(b, nc, chunk, *cm.shape[2:])
    la = jnp.cumsum(dt * a, axis=2)
    xdt = xs * dt[..., None]
    causal = jnp.tril(jnp.ones((chunk, chunk), dtype=bool))[:, :, None, None]
    seg = la[:, :, :, None] - la[:, :, None, :]
    decay = jnp.exp(jnp.where(causal, seg, -jnp.inf))
    cb = jnp.einsum('bclgn,bcsgn->bclsg', cm, bm)
    y = jnp.einsum('bclsgh,bcsghp->bclghp', cb[..., None] * decay, xdt)
    decay_end = jnp.exp(la[:, :, -1:] - la)
    states = jnp.einsum('bcsgn,bcsghp->bcghpn', bm, xdt * decay_end[..., None])
    chunk_decay = jnp.exp(la[:, :, -1])

    def carry_step(h, inp):
        s_c, d_c = inp
        return h * d_c[..., None, None] + s_c, h

    h_last, h_prev = lax.scan(carry_step, h0,
                              (jnp.moveaxis(states, 1, 0), jnp.moveaxis(chunk_decay, 1, 0)))
    h_prev = jnp.moveaxis(h_prev, 0, 1)
    y = y + jnp.einsum('bclgn,bcghpn->bclghp', cm, h_prev) * jnp.exp(la)[..., None]
    return y.reshape(b, t, *y.shape[3:]), h_last


def ssd_mixer(z, xbc, dt_raw, conv_past, h0, conv_w, conv_b, dt_bias, a_log, d_skip, norm_g):
    b, t = z.shape[:2]
    xbc, conv_new = causal_dwconv(xbc, conv_past, conv_w, conv_b)
    xbc = jax.nn.silu(xbc.astype(jnp.float32))
    xs, bm, cm = jnp.split(xbc, (D_SSD, D_SSD + SSD_GROUPS * D_STATE), axis=-1)
    dt = jax.nn.softplus(dt_raw.astype(jnp.float32) + dt_bias.astype(jnp.float32))
    a = -jnp.exp(a_log.astype(jnp.float32))
    chunk = SSD_CHUNK if t % SSD_CHUNK == 0 else t
    xs = xs.reshape(b, t, SSD_GROUPS, HEADS_PER_GROUP, SSD_HEAD_DIM)
    y, h_last = ssd_scan(
        xs,
        dt.reshape(b, t, SSD_GROUPS, HEADS_PER_GROUP),
        a.reshape(SSD_GROUPS, HEADS_PER_GROUP),
        bm.reshape(b, t, SSD_GROUPS, D_STATE),
        cm.reshape(b, t, SSD_GROUPS, D_STATE),
        h0.astype(jnp.float32).reshape(b, SSD_GROUPS, HEADS_PER_GROUP, SSD_HEAD_DIM, D_STATE),
        chunk)
    y = y + xs * d_skip.astype(jnp.float32).reshape(SSD_GROUPS, HEADS_PER_GROUP, 1)
    y = y.reshape(b, t, D_SSD) * jax.nn.silu(z.astype(jnp.float32))
    y = rms_norm(y, norm_g).astype(z.dtype)
    h_last = h_last.reshape(b, SSD_HEADS, SSD_HEAD_DIM, D_STATE).astype(h0.dtype)
    return y, conv_new, h_last


def memory_kv(mem, g_mem, w_ck, w_cv, ck_norm_g):
    b, m = mem.shape[:2]
    h = rms_norm(mem, g_mem)
    k = rms_norm((h @ w_ck).reshape(b, m, MEM_HEADS, MEM_HEAD_DIM), ck_norm_g)
    v = (h @ w_cv).reshape(b, m, MEM_HEADS, MEM_HEAD_DIM)
    return k, v


def cross_attend(x, mem_k, mem_v, g_cross, w_cq, cq_norm_g, w_co):
    b, t = x.shape[:2]
    q = rms_norm((rms_norm(x, g_cross) @ w_cq).reshape(b, t, MEM_HEADS, MEM_HEAD_DIM), cq_norm_g)
    logits = jnp.einsum('bqhd,bshd->bhqs', q, mem_k.astype(q.dtype),
                        preferred_element_type=jnp.float32) * MEM_HEAD_DIM ** -0.5
    p = jax.nn.softmax(logits, axis=-1)
    o = jnp.einsum('bhqs,bshd->bqhd', p.astype(x.dtype), mem_v.astype(x.dtype))
    return o.reshape(b, t, D_XATTN) @ w_co


def conv_ffn(x, past, g_ffn, w_up, conv_w, conv_b, w_down):
    u, new_past = causal_dwconv(rms_norm(x, g_ffn) @ w_up, past, conv_w, conv_b)
    gate, val = jnp.split(u, 2, axis=-1)
    return (jax.nn.silu(gate) * val) @ w_down, new_past


def _layer(x, mem_k, mem_v, k_buf, v_buf, ssd_conv_past, ssd_h0, ffn_past, w):
    b, t = x.shape[:2]
    proj = rms_norm(x, w['g_mix']) @ w['w_in']
    q, k, v, z, xbc, dt_raw = jnp.split(proj, IN_SPLITS, axis=-1)
    q = rms_norm(q.reshape(b, t, N_HEADS, HEAD_DIM), w['q_norm_g'])
    k = rms_norm(k.reshape(b, t, N_KV_HEADS, HEAD_DIM), w['k_norm_g'])
    v = v.reshape(b, t, N_KV_HEADS, HEAD_DIM)
    if k_buf is None:
        attn = swa_banded(q, k, v, w['sinks'])
        new_k, new_v = k[:, -WINDOW:], v[:, -WINDOW:]
    else:
        attn, new_k, new_v = swa_step(q, k, v, k_buf, v_buf, w['sinks'])
    ssd, new_conv, new_h = ssd_mixer(z, xbc, dt_raw, ssd_conv_past, ssd_h0, w['ssd_conv_w'],
                                     w['ssd_conv_b'], w['dt_bias'], w['a_log'], w['d_skip'],
                                     w['ssd_norm_g'])
    x = x + jnp.concatenate([attn, ssd], axis=-1) @ w['w_out']
    x = x + cross_attend(x, mem_k, mem_v, w['g_cross'], w['w_cq'], w['cq_norm_g'], w['w_co'])
    f, new_ffn = conv_ffn(x, ffn_past, w['g_ffn'], w['w_up'], w['ffn_conv_w'], w['ffn_conv_b'],
                          w['w_down'])
    return x + f, (new_k, new_v, new_conv, new_h, new_ffn)


def setup_inputs(seed: int = 0) -> dict:
    key = jax.random.key(seed)
    keys = iter(jax.random.split(key, 48))
    L = DEPTH
    swa_buf = min(WINDOW, PAST_LEN)

    def nrm(shape, scale=1.0):
        return jax.random.normal(next(keys), shape, jnp.float32) * scale

    def gain(n):
        return 1.0 + nrm((L, n), 0.05)

    dt0 = jnp.exp(jax.random.uniform(next(keys), (L, SSD_HEADS), jnp.float32,
                                     math.log(1e-3), math.log(1e-1)))
    dt_bias = dt0 + jnp.log(-jnp.expm1(-dt0))
    a_log = jnp.log(jax.random.uniform(next(keys), (L, SSD_HEADS), jnp.float32, 1.0, 16.0))
    return {
        'x_prompt': nrm((BATCH, SEQ, D_MODEL)),
        'x_sample': nrm((DEC_BATCH, DEC_SEQ, D_MODEL)),
        'cache_swa_k': nrm((L, DEC_BATCH, swa_buf, N_KV_HEADS, HEAD_DIM)),
        'cache_swa_v': nrm((L, DEC_BATCH, swa_buf, N_KV_HEADS, HEAD_DIM)),
        'state_ssd_conv': nrm((L, DEC_BATCH, SSD_CONV - 1, CONV_DIM)),
        'state_ssd': nrm((L, DEC_BATCH, SSD_HEADS, SSD_HEAD_DIM, D_STATE), 0.5),
        'cache_mem_k': nrm((L, DEC_BATCH, N_MEM, MEM_HEADS, MEM_HEAD_DIM)),
        'cache_mem_v': nrm((L, DEC_BATCH, N_MEM, MEM_HEADS, MEM_HEAD_DIM)),
        'state_ffn_conv': nrm((L, DEC_BATCH, FFN_CONV - 1, 2 * D_FF)),
        'mem_prompt': nrm((BATCH, N_MEM, D_MODEL)),
        'g_mix': gain(D_MODEL),
        'w_in': nrm((L, D_MODEL, D_IN), D_MODEL ** -0.5),
        'q_norm_g': gain(HEAD_DIM),
        'k_norm_g': gain(HEAD_DIM),
        'sinks': nrm((L, N_HEADS)),
        'ssd_conv_w': nrm((L, SSD_CONV, CONV_DIM), SSD_CONV ** -0.5),
        'ssd_conv_b': nrm((L, CONV_DIM), 0.02),
        'dt_bias': dt_bias,
        'a_log': a_log,
        'd_skip': 1.0 + nrm((L, SSD_HEADS), 0.1),
        'ssd_norm_g': gain(D_SSD),
        'w_out': nrm((L, D_MIX, D_MODEL), D_MIX ** -0.5),
        'g_cross': gain(D_MODEL),
        'g_mem': gain(D_MODEL),
        'w_cq': nrm((L, D_MODEL, D_XATTN), D_MODEL ** -0.5),
        'w_ck': nrm((L, D_MODEL, D_XATTN), D_MODEL ** -0.5),
        'w_cv': nrm((L, D_MODEL, D_XATTN), D_MODEL ** -0.5),
        'cq_norm_g': gain(MEM_HEAD_DIM),
        'ck_norm_g': gain(MEM_HEAD_DIM),
        'w_co': nrm((L, D_XATTN, D_MODEL), D_XATTN ** -0.5),
        'g_ffn': gain(D_MODEL),
        'w_up': nrm((L, D_MODEL, 2 * D_FF), D_MODEL ** -0.5),
        'ffn_conv_w': nrm((L, FFN_CONV, 2 * D_FF), FFN_CONV ** -0.5),
        'ffn_conv_b': nrm((L, 2 * D_FF), 0.02),
        'w_down': nrm((L, D_FF, D_MODEL), D_FF ** -0.5),
    }


def reference(x_prompt, x_sample, cache_swa_k, cache_swa_v, state_ssd_conv, state_ssd,
              cache_mem_k, cache_mem_v, state_ffn_conv, mem_prompt,
              g_mix, w_in, q_norm_g, k_norm_g, sinks, ssd_conv_w, ssd_conv_b, dt_bias, a_log,
              d_skip, ssd_norm_g, w_out, g_cross, g_mem, w_cq, w_ck, w_cv, cq_norm_g, ck_norm_g,
              w_co, g_ffn, w_up, ffn_conv_w, ffn_conv_b, w_down):
    y_prompt, y_sample = x_prompt, x_sample
    bp = x_prompt.shape[0]
    new_p, new_s = [], []
    for l in range(DEPTH):
        w = dict(g_mix=g_mix[l], w_in=w_in[l], q_norm_g=q_norm_g[l], k_norm_g=k_norm_g[l],
                 sinks=sinks[l], ssd_conv_w=ssd_conv_w[l], ssd_conv_b=ssd_conv_b[l],
                 dt_bias=dt_bias[l], a_log=a_log[l], d_skip=d_skip[l], ssd_norm_g=ssd_norm_g[l],
                 w_out=w_out[l], g_cross=g_cross[l], w_cq=w_cq[l], cq_norm_g=cq_norm_g[l],
                 w_co=w_co[l], g_ffn=g_ffn[l], w_up=w_up[l], ffn_conv_w=ffn_conv_w[l],
                 ffn_conv_b=ffn_conv_b[l], w_down=w_down[l])
        mk_p, mv_p = memory_kv(mem_prompt, g_mem[l], w_ck[l], w_cv[l], ck_norm_g[l])
        dt_x = x_prompt.dtype
        y_prompt, st_p = _layer(
            y_prompt, mk_p, mv_p, None, None,
            jnp.zeros((bp, SSD_CONV - 1, CONV_DIM), dt_x),
            jnp.zeros((bp, SSD_HEADS, SSD_HEAD_DIM, D_STATE), dt_x),
            jnp.zeros((bp, FFN_CONV - 1, 2 * D_FF), dt_x), w)
        y_sample, st_s = _layer(
            y_sample, cache_mem_k[l], cache_mem_v[l], cache_swa_k[l], cache_swa_v[l],
            state_ssd_conv[l], state_ssd[l], state_ffn_conv[l], w)
        new_p.append(st_p + (mk_p, mv_p))
        new_s.append(st_s)
    (swa_k_prompt, swa_v_prompt, ssd_conv_prompt, ssd_state_prompt, ffn_conv_prompt,
     mem_k_prompt, mem_v_prompt) = [jnp.stack(a) for a in zip(*new_p)]
    (swa_k_sample, swa_v_sample, ssd_conv_sample, ssd_state_sample,
     ffn_conv_sample) = [jnp.stack(a) for a in zip(*new_s)]
    return (y_prompt, y_sample, swa_k_prompt, swa_v_prompt, swa_k_sample, swa_v_sample,
            ssd_conv_prompt, ssd_conv_sample, ssd_state_prompt, ssd_state_sample,
            mem_k_prompt, mem_v_prompt, ffn_conv_prompt, ffn_conv_sample)
```

```python
import functools
import math

import numpy as np
import jax
import jax.numpy as jnp
from jax import lax
from jax.experimental import pallas as pl
from jax.experimental.pallas import tpu as pltpu

F32 = jnp.float32
BF16 = jnp.bfloat16

D_MODEL = 2048
BATCH = 4
SEQ = 2048
DEC_BATCH = 128
N_HEADS = 16
N_KV = 4
Q_PER_KV = 4
HEAD_DIM = 64
D_ATTN = N_HEADS * HEAD_DIM
D_KV = N_KV * HEAD_DIM
WINDOW = 128
SSD_HEADS = 32
SSD_P = 64
D_SSD = SSD_HEADS * SSD_P
SSD_GROUPS = 4
HEADS_PER_GROUP = 8
D_STATE = 128
D_BC = SSD_GROUPS * D_STATE
SSD_CONV = 4
CHUNK = 128
CONV_DIM = D_SSD + 2 * D_BC
N_MEM = 256
MEM_HEADS = 4
MEM_HD = 128
D_XATTN = 512
D_FF = 5632
FFN_CONV = 3
EPS = 1e-6

LANES = 128
SUBLANES = 8
BF16_ROWS = 16
VMEM_LIMIT = 56 * 1024 * 1024

OFF_Z = 0
OFF_XS = 2048
OFF_Q = 4096
OFF_BC = 5120
OFF_K = 6144
OFF_V = 6400
OFF_DT = 6656
N_PROJ = 6912
TN_PROJ = 768


def _cparams(sem):
    return pltpu.CompilerParams(dimension_semantics=sem, vmem_limit_bytes=VMEM_LIMIT)


def _rms(x, g):
    return x * lax.rsqrt(jnp.mean(x * x, axis=-1, keepdims=True) + EPS) * g


def _dot(a, b):
    return jnp.dot(a, b, preferred_element_type=F32)


def _dot_nt(a, b):
    return lax.dot_general(a, b, (((1,), (1,)), ((), ())), preferred_element_type=F32)


def _dot_tn(a, b):
    return lax.dot_general(a, b, (((0,), (0,)), ((), ())), preferred_element_type=F32)


def _split3(x):
    hi = x.astype(BF16)
    r = x - hi.astype(F32)
    mid = r.astype(BF16)
    lo = (r - mid.astype(F32)).astype(BF16)
    return hi, mid, lo


def _hi_lo(x):
    hi = x.astype(BF16).astype(F32)
    return hi, x - hi


def _dot_x01(x, w01):
    hi, mid, lo = _split3(x)
    return _dot(hi, w01) + _dot(mid, w01) + _dot(lo, w01)


def _dot_01x(w01, x):
    hi, mid, lo = _split3(x)
    return _dot(w01, hi) + _dot(w01, mid) + _dot(w01, lo)


def _softplus(x):
    return jnp.maximum(x, 0.0) + jnp.log1p(jnp.exp(-jnp.abs(x)))


def _silu(x):
    return x * jax.nn.sigmoid(x)


def _shift_rows(cur, tail8, k):
    rolled = pltpu.roll(cur, k, 0)
    row8 = lax.broadcasted_iota(jnp.int32, tail8.shape, 0)
    top = jnp.where(row8 < k, pltpu.roll(tail8, k, 0), rolled[0:SUBLANES])
    return jnp.concatenate([top, rolled[SUBLANES:]], axis=0)


def _norm_matmul_kernel(x_ref, g_ref, w_ref, o_ref, xn_ref, *, row_chunk):
    @pl.when(pl.program_id(1) == 0)
    def _():
        def body(r, c):
            sl = pl.ds(pl.multiple_of(r * row_chunk, row_chunk), row_chunk)
            xn_ref[sl, :] = _rms(x_ref[sl, :], g_ref[...]).astype(BF16)
            return c

        lax.fori_loop(0, x_ref.shape[0] // row_chunk, body, 0)

    o_ref[...] = _dot(xn_ref[...], w_ref[...])


def _norm_matmul(x, g, w, tm, tn):
    m, k = x.shape
    n = w.shape[1]
    return pl.pallas_call(
        functools.partial(_norm_matmul_kernel, row_chunk=min(tm, 128)),
        grid=(m // tm, n // tn),
        in_specs=[
            pl.BlockSpec((tm, k), lambda i, j: (i, 0)),
            pl.BlockSpec((1, k), lambda i, j: (0, 0)),
            pl.BlockSpec((k, tn), lambda i, j: (0, j)),
        ],
        out_specs=pl.BlockSpec((tm, tn), lambda i, j: (i, j)),
        out_shape=jax.ShapeDtypeStruct((m, n), F32),
        scratch_shapes=[pltpu.VMEM((tm, k), BF16)],
        compiler_params=_cparams(("parallel", "arbitrary")),
        name="norm_in_proj",
    )(x, g, w)


def _swa_prompt_kernel(sink_ref, q_ref, kc_ref, vc_ref, kp_ref, vp_ref, gq_ref, gk_ref, o_ref, kout_ref):
    n = pl.program_id(1)
    nb = pl.num_programs(1)
    gk = gk_ref[...]
    gq = gq_ref[...]

    def k_norm(k):
        return jnp.concatenate([_rms(k[:, h * HEAD_DIM:(h + 1) * HEAD_DIM], gk) for h in range(N_KV)], axis=1)

    kn = k_norm(kc_ref[...])

    @pl.when(n == nb - 1)
    def _():
        kout_ref[0] = kn

    kcur = kn.astype(BF16)
    kprev = k_norm(kp_ref[...]).astype(BF16)
    vcur = vc_ref[...].astype(BF16)
    vprev = vp_ref[...].astype(BF16)
    row = lax.broadcasted_iota(jnp.int32, (WINDOW, WINDOW), 0)
    col = lax.broadcasted_iota(jnp.int32, (WINDOW, WINDOW), 1)
    mask_cur = col <= row
    mask_prev = col >= jnp.where(n > 0, row, WINDOW)
    scale = HEAD_DIM ** -0.5
    q = q_ref[...]
    parts = [None] * N_HEADS
    for kv in range(N_KV):
        sl = slice(kv * HEAD_DIM, (kv + 1) * HEAD_DIM)
        for g in range(Q_PER_KV):
            c0 = g * D_KV + kv * HEAD_DIM
            qh = _rms(q[:, c0:c0 + HEAD_DIM], gq).astype(BF16)
            lc = jnp.where(mask_cur, _dot_nt(qh, kcur[:, sl]) * scale, -jnp.inf)
            lp = jnp.where(mask_prev, _dot_nt(qh, kprev[:, sl]) * scale, -jnp.inf)
            sink = sink_ref[kv * Q_PER_KV + g]
            mx = jnp.maximum(jnp.maximum(jnp.max(lc, axis=-1, keepdims=True), jnp.max(lp, axis=-1, keepdims=True)),
                             sink)
            pc = jnp.exp(lc - mx)
            pp = jnp.exp(lp - mx)
            den = jnp.sum(pc, axis=-1, keepdims=True) + jnp.sum(pp, axis=-1, keepdims=True) + jnp.exp(sink - mx)
            parts[g * N_KV + kv] = (_dot((pc / den).astype(BF16), vcur[:, sl])
                                    + _dot((pp / den).astype(BF16), vprev[:, sl]))
    o_ref[...] = jnp.concatenate(parts, axis=1).astype(BF16)


def _swa_prompt(proj, sinks, gq, gk):
    nb = SEQ // WINDOW
    qb, kb, vb = OFF_Q // D_ATTN, OFF_K // D_KV, OFF_V // D_KV
    cur = lambda b, n: b * (SEQ // WINDOW) + n
    prev = lambda b, n: b * (SEQ // WINDOW) + jnp.maximum(n - 1, 0)
    return pl.pallas_call(
        _swa_prompt_kernel,
        grid=(BATCH, nb),
        in_specs=[
            pl.BlockSpec(memory_space=pltpu.SMEM),
            pl.BlockSpec((WINDOW, D_ATTN), lambda b, n: (cur(b, n), qb)),
            pl.BlockSpec((WINDOW, D_KV), lambda b, n: (cur(b, n), kb)),
            pl.BlockSpec((WINDOW, D_KV), lambda b, n: (cur(b, n), vb)),
            pl.BlockSpec((WINDOW, D_KV), lambda b, n: (prev(b, n), kb)),
            pl.BlockSpec((WINDOW, D_KV), lambda b, n: (prev(b, n), vb)),
            pl.BlockSpec((1, HEAD_DIM), lambda b, n: (0, 0)),
            pl.BlockSpec((1, HEAD_DIM), lambda b, n: (0, 0)),
        ],
        out_specs=[
            pl.BlockSpec((WINDOW, D_ATTN), lambda b, n: (cur(b, n), 0)),
            pl.BlockSpec((1, WINDOW, D_KV), lambda b, n: (b, 0, 0)),
        ],
        out_shape=[
            jax.ShapeDtypeStruct((BATCH * SEQ, D_ATTN), BF16),
            jax.ShapeDtypeStruct((BATCH, WINDOW, D_KV), F32),
        ],
        compiler_params=_cparams(("parallel", "arbitrary")),
        name="swa_prompt",
    )(sinks, proj, proj, proj, proj, proj, gq, gk)


def _conv4(cur, tail8, w, b):
    acc = cur * w[3:4] + b
    for k in (1, 2, 3):
        acc = acc + _shift_rows(cur, tail8, k) * w[3 - k:4 - k]
    return acc


def _ssd_prompt_kernel(z_ref, xs_ref, bc_ref, dt_ref, cwx_ref, cbx_ref, cwb_ref, cbb_ref, dtb_ref, alog_ref,
                       dskip_ref, ng_ref, e_ref, y_ref, st_ref, s_ref, tx_ref, tb_ref):
    c = pl.program_id(1)
    nc = pl.num_programs(1)

    @pl.when(c == 0)
    def _():
        s_ref[...] = jnp.zeros_like(s_ref)
        tx_ref[...] = jnp.zeros_like(tx_ref)
        tb_ref[...] = jnp.zeros_like(tb_ref)

    xs_raw = xs_ref[...]
    bc_raw = bc_ref[...]
    xs = _silu(_conv4(xs_raw, tx_ref[...], cwx_ref[...], cbx_ref[...]))
    bc = _silu(_conv4(bc_raw, tb_ref[...], cwb_ref[...], cbb_ref[...]))
    tx_ref[...] = xs_raw[CHUNK - SUBLANES:]
    tb_ref[...] = bc_raw[CHUNK - SUBLANES:]

    e01 = e_ref[...]
    dtv = _softplus(dt_ref[...] + dtb_ref[...])
    da = dtv * (-jnp.exp(alog_ref[...]))
    row = lax.broadcasted_iota(jnp.int32, (CHUNK, CHUNK), 0)
    col = lax.broadcasted_iota(jnp.int32, (CHUNK, CHUNK), 1)
    causal = row >= col
    tri01 = jnp.where(causal, 1.0, 0.0).astype(BF16)
    la = _dot_01x(tri01, da)
    la_t = la.T
    la_x = _dot_x01(la, e01)
    dt_x = _dot_x01(dtv, e01)
    la_end_x = la_x[CHUNK - 1:CHUNK]
    xdt = xs * dt_x
    xdt_b = xdt.astype(BF16)
    xw_b = (xdt * jnp.exp(la_end_x - la_x)).astype(BF16)
    chunk_decay_x = jnp.exp(la_end_x)

    lane = lax.broadcasted_iota(jnp.int32, (CHUNK, LANES), 1)
    lo_half = lane < SSD_P
    zero_b = jnp.zeros((CHUNK, LANES), BF16)
    y_parts = []
    yi_parts = []
    for g in range(SSD_GROUPS):
        bg = bc[:, g * D_STATE:(g + 1) * D_STATE].astype(BF16)
        cg = bc[:, D_BC + g * D_STATE:D_BC + (g + 1) * D_STATE].astype(BF16)
        cb = _dot_nt(cg, bg)
        for j in range(HEADS_PER_GROUP // 2):
            ms = []
            for h in (g * HEADS_PER_GROUP + 2 * j, g * HEADS_PER_GROUP + 2 * j + 1):
                seg = la[:, h:h + 1] - la_t[h:h + 1, :]
                ms.append((cb * jnp.exp(jnp.where(causal, seg, -jnp.inf))).astype(BF16))
            c0 = (g * HEADS_PER_GROUP + 2 * j) * SSD_P
            xp = xdt_b[:, c0:c0 + LANES]
            rhs = jnp.concatenate([jnp.where(lo_half, xp, zero_b), jnp.where(lo_half, zero_b, xp)], axis=0)
            y_parts.append(_dot(jnp.concatenate(ms, axis=1), rhs))
        gs = slice(g * HEADS_PER_GROUP * SSD_P, (g + 1) * HEADS_PER_GROUP * SSD_P)
        s_g = s_ref[:, gs]
        yi_parts.append(_dot(cg, s_g.astype(BF16)))
        s_ref[:, gs] = s_g * chunk_decay_x[:, gs] + _dot_tn(bg, xw_b[:, gs])
    y = jnp.concatenate(y_parts, axis=1) + jnp.concatenate(yi_parts, axis=1) * jnp.exp(la_x)
    y = y + xs * dskip_ref[...]
    y = y * _silu(z_ref[...])
    y_ref[...] = _rms(y, ng_ref[...]).astype(BF16)

    @pl.when(c == nc - 1)
    def _():
        st_ref[0] = s_ref[...]


def _ssd_prompt(proj, cwx, cbx, cwb, cbb, dtb, alog, dskip_x, ng, e01):
    nc = SEQ // CHUNK
    rowblk = lambda b, c: b * (SEQ // CHUNK) + c
    const = lambda b, c: (0, 0)
    return pl.pallas_call(
        _ssd_prompt_kernel,
        grid=(BATCH, nc),
        in_specs=[
            pl.BlockSpec((CHUNK, D_SSD), lambda b, c: (rowblk(b, c), OFF_Z // D_SSD)),
            pl.BlockSpec((CHUNK, D_SSD), lambda b, c: (rowblk(b, c), OFF_XS // D_SSD)),
            pl.BlockSpec((CHUNK, 2 * D_BC), lambda b, c: (rowblk(b, c), OFF_BC // (2 * D_BC))),
            pl.BlockSpec((CHUNK, LANES), lambda b, c: (rowblk(b, c), OFF_DT // LANES)),
            pl.BlockSpec((SSD_CONV, D_SSD), const),
            pl.BlockSpec((1, D_SSD), const),
            pl.BlockSpec((SSD_CONV, 2 * D_BC), const),
            pl.BlockSpec((1, 2 * D_BC), const),
            pl.BlockSpec((1, LANES), const),
            pl.BlockSpec((1, LANES), const),
            pl.BlockSpec((1, D_SSD), const),
            pl.BlockSpec((1, D_SSD), const),
            pl.BlockSpec((LANES, D_SSD), const),
        ],
        out_specs=[
            pl.BlockSpec((CHUNK, D_SSD), lambda b, c: (rowblk(b, c), 0)),
            pl.BlockSpec((1, D_STATE, D_SSD), lambda b, c: (b, 0, 0)),
        ],
        out_shape=[
            jax.ShapeDtypeStruct((BATCH * SEQ, D_SSD), BF16),
            jax.ShapeDtypeStruct((BATCH, D_STATE, D_SSD), F32),
        ],
        scratch_shapes=[
            pltpu.VMEM((D_STATE, D_SSD), F32),
            pltpu.VMEM((SUBLANES, D_SSD), F32),
            pltpu.VMEM((SUBLANES, 2 * D_BC), F32),
        ],
        compiler_params=_cparams(("parallel", "arbitrary")),
        name="ssd_prompt",
    )(proj, proj, proj, proj, cwx, cbx, cwb, cbb, dtb, alog, dskip_x, ng, e01)


def _out_proj_kernel(a_ref, s_ref, wa_ref, ws_ref, x_ref, o_ref):
    o_ref[...] = x_ref[...] + _dot(a_ref[...], wa_ref[...]) + _dot(s_ref[...], ws_ref[...])


def _out_proj(attn, ssd, wa, ws, x, tm, tn):
    m = x.shape[0]
    return pl.pallas_call(
        _out_proj_kernel,
        grid=(m // tm, D_MODEL // tn),
        in_specs=[
            pl.BlockSpec((tm, D_ATTN), lambda i, j: (i, 0)),
            pl.BlockSpec((tm, D_SSD), lambda i, j: (i, 0)),
            pl.BlockSpec((D_ATTN, tn), lambda i, j: (0, j)),
            pl.BlockSpec((D_SSD, tn), lambda i, j: (0, j)),
            pl.BlockSpec((tm, tn), lambda i, j: (i, j)),
        ],
        out_specs=pl.BlockSpec((tm, tn), lambda i, j: (i, j)),
        out_shape=jax.ShapeDtypeStruct((m, D_MODEL), F32),
        compiler_params=_cparams(("parallel", "parallel")),
        name="out_proj",
    )(attn, ssd, wa, ws, x)


def _mem_kv_kernel(m_ref, g_ref, w_ref, gk_ref, k_ref, v_ref):
    h = _rms(m_ref[...], g_ref[...]).astype(BF16)
    kv = _dot(h, w_ref[...])
    gk = gk_ref[...]
    k_ref[...] = jnp.concatenate(
        [_rms(kv[:, i * MEM_HD:(i + 1) * MEM_HD], gk) for i in range(MEM_HEADS)], axis=1)
    v_ref[...] = kv[:, D_XATTN:]


def _mem_kv(mem, g_mem, wkv, gk):
    m = mem.shape[0]
    tm = 256
    return pl.pallas_call(
        _mem_kv_kernel,
        grid=(m // tm,),
        in_specs=[
            pl.BlockSpec((tm, D_MODEL), lambda i: (i, 0)),
            pl.BlockSpec((1, D_MODEL), lambda i: (0, 0)),
            pl.BlockSpec((D_MODEL, 2 * D_XATTN), lambda i: (0, 0)),
            pl.BlockSpec((1, MEM_HD), lambda i: (0, 0)),
        ],
        out_specs=[
            pl.BlockSpec((tm, D_XATTN), lambda i: (i, 0)),
            pl.BlockSpec((tm, D_XATTN), lambda i: (i, 0)),
        ],
        out_shape=[jax.ShapeDtypeStruct((m, D_XATTN), F32), jax.ShapeDtypeStruct((m, D_XATTN), F32)],
        compiler_params=_cparams(("parallel",)),
        name="mem_kv",
    )(mem, g_mem, wkv, gk)


def _cross_prompt_kernel(x_ref, g_ref, wq_ref, gq_ref, mk_ref, mv_ref, wo_ref, o_ref):
    x = x_ref[...]
    q = _dot(_rms(x, g_ref[...]).astype(BF16), wq_ref[...])
    gq = gq_ref[...]
    mk = mk_ref[...].astype(BF16)
    mv = mv_ref[...].astype(BF16)
    scale = MEM_HD ** -0.5
    outs = []
    for h in range(MEM_HEADS):
        sl = slice(h * MEM_HD, (h + 1) * MEM_HD)
        qh = _rms(q[:, sl], gq).astype(BF16)
        logits = _dot_nt(qh, mk[:, sl]) * scale
        mx = jnp.max(logits, axis=-1, keepdims=True)
        p = jnp.exp(logits - mx)
        p = p / jnp.sum(p, axis=-1, keepdims=True)
        outs.append(_dot(p.astype(BF16), mv[:, sl]))
    o = jnp.concatenate(outs, axis=1).astype(BF16)
    o_ref[...] = x + _dot(o, wo_ref[...])


def _cross_prompt(x, g, wq, gq, mk, mv, wo, tm):
    m = x.shape[0]
    per_b = SEQ // tm
    return pl.pallas_call(
        _cross_prompt_kernel,
        grid=(m // tm,),
        in_specs=[
            pl.BlockSpec((tm, D_MODEL), lambda i: (i, 0)),
            pl.BlockSpec((1, D_MODEL), lambda i: (0, 0)),
            pl.BlockSpec((D_MODEL, D_XATTN), lambda i: (0, 0)),
            pl.BlockSpec((1, MEM_HD), lambda i: (0, 0)),
            pl.BlockSpec((N_MEM, D_XATTN), lambda i: (i // per_b, 0)),
            pl.BlockSpec((N_MEM, D_XATTN), lambda i: (i // per_b, 0)),
            pl.BlockSpec((D_XATTN, D_MODEL), lambda i: (0, 0)),
        ],
        out_specs=pl.BlockSpec((tm, D_MODEL), lambda i: (i, 0)),
        out_shape=jax.ShapeDtypeStruct((m, D_MODEL), F32),
        compiler_params=_cparams(("parallel",)),
        name="cross_prompt",
    )(x, g, wq, gq, mk, mv, wo)


def _ffn_prompt_kernel(x_ref, halo_ref, g_ref, wg_ref, wv_ref, cwg_ref, cbg_ref, cwv_ref, cbv_ref, wd_ref,
                       o_ref, tg_ref, tv_ref, xn_ref, *, tm, tiles_per_seq):
    i = pl.program_id(0)
    c = pl.program_id(1)
    halo = BF16_ROWS

    @pl.when(c == 0)
    def _():
        g = g_ref[...]
        first = (i % tiles_per_seq) == 0
        hn = _rms(halo_ref[...], g)
        xn_ref[0:halo, :] = jnp.where(first, jnp.zeros_like(hn), hn).astype(BF16)

        def body(r, carry):
            sl = pl.ds(pl.multiple_of(r * 128, 128), 128)
            xn_ref[pl.ds(pl.multiple_of(halo + r * 128, BF16_ROWS), 128), :] = _rms(x_ref[sl, :], g).astype(BF16)
            return carry

        lax.fori_loop(0, tm // 128, body, 0)
        o_ref[...] = x_ref[...]

    xn = xn_ref[...]

    def conv(u, cw, cb):
        acc = u * cw[2:3] + cb
        acc = acc + pltpu.roll(u, 1, 0) * cw[1:2]
        acc = acc + pltpu.roll(u, 2, 0) * cw[0:1]
        return acc[halo:]

    ug = _dot(xn, wg_ref[...])
    uv = _dot(xn, wv_ref[...])
    tg_ref[0] = ug[halo + tm - SUBLANES:]
    tv_ref[0] = uv[halo + tm - SUBLANES:]
    hcv = _silu(conv(ug, cwg_ref[...], cbg_ref[...])) * conv(uv, cwv_ref[...], cbv_ref[...])
    o_ref[...] += _dot(hcv.astype(BF16), wd_ref[...])


def _ffn_prompt(x, g, wup, cw, cb, wd, tm, tc):
    m = x.shape[0]
    ncol = D_FF // tc
    halo = BF16_ROWS
    tiles = m // tm
    return pl.pallas_call(
        functools.partial(_ffn_prompt_kernel, tm=tm, tiles_per_seq=SEQ // tm),
        grid=(tiles, ncol),
        in_specs=[
            pl.BlockSpec((tm, D_MODEL), lambda i, c: (i, 0)),
            pl.BlockSpec((halo, D_MODEL), lambda i, c: (jnp.maximum(i * (tm // halo) - 1, 0), 0)),
            pl.BlockSpec((1, D_MODEL), lambda i, c: (0, 0)),
            pl.BlockSpec((D_MODEL, tc), lambda i, c: (0, c)),
            pl.BlockSpec((D_MODEL, tc), lambda i, c: (0, ncol + c)),
            pl.BlockSpec((FFN_CONV, tc), lambda i, c: (0, c)),
            pl.BlockSpec((1, tc), lambda i, c: (0, c)),
            pl.BlockSpec((FFN_CONV, tc), lambda i, c: (0, ncol + c)),
            pl.BlockSpec((1, tc), lambda i, c: (0, ncol + c)),
            pl.BlockSpec((tc, D_MODEL), lambda i, c: (c, 0)),
        ],
        out_specs=[
            pl.BlockSpec((tm, D_MODEL), lambda i, c: (i, 0)),
            pl.BlockSpec((1, SUBLANES, tc), lambda i, c: (i, 0, c)),
            pl.BlockSpec((1, SUBLANES, tc), lambda i, c: (i, 0, c)),
        ],
        out_shape=[
            jax.ShapeDtypeStruct((m, D_MODEL), F32),
            jax.ShapeDtypeStruct((tiles, SUBLANES, D_FF), F32),
            jax.ShapeDtypeStruct((tiles, SUBLANES, D_FF), F32),
        ],
        scratch_shapes=[pltpu.VMEM((halo + tm, D_MODEL), BF16)],
        compiler_params=_cparams(("parallel", "arbitrary")),
        name="ffn_prompt",
    )(x, x, g, wup, wup, cw, cb, cw, cb, wd)


def _seg_norm(x, seg01, g):
    ss = _dot_x01(x * x, seg01)
    return x * lax.rsqrt(ss * (1.0 / HEAD_DIM) + EPS) * g


def _swa_step_kernel(q_ref, k_ref, v_ref, ck_ref, cv_ref, gq_ref, gk_ref, sink_ref, seg_ref, o_ref, ok_ref, ov_ref,
                     *, bb):
    seg01 = seg_ref[...]
    gq = gq_ref[...]
    kn = _seg_norm(k_ref[...], seg01, gk_ref[...])
    vn = v_ref[...]
    q = q_ref[...]
    qn = [_seg_norm(q[:, g * D_KV:(g + 1) * D_KV], seg01, gq) for g in range(Q_PER_KV)]
    rr = lax.broadcasted_iota(jnp.int32, (SUBLANES, D_KV), 0)
    ll = lax.broadcasted_iota(jnp.int32, (SUBLANES, D_KV), 1)
    diag = rr == ll // HEAD_DIM
    sink = sink_ref[...]
    scale = HEAD_DIM ** -0.5
    for b in range(bb):
        ck = ck_ref[b]
        cv = cv_ref[b]
        knew = kn[b:b + 1]
        vnew = vn[b:b + 1]
        ok_ref[b, 0:WINDOW - 1, :] = ck[1:]
        ok_ref[b, WINDOW - 1:WINDOW, :] = knew
        ov_ref[b, 0:WINDOW - 1, :] = cv[1:]
        ov_ref[b, WINDOW - 1:WINDOW, :] = vnew
        qexp = jnp.concatenate(
            [jnp.where(diag, jnp.broadcast_to(qn[g][b:b + 1], (SUBLANES, D_KV)), 0.0) for g in range(Q_PER_KV)],
            axis=0)
        qexp_b = qexp.astype(BF16)
        logits = _dot_nt(qexp_b, ck.astype(BF16)) * scale
        lnew = jnp.sum(qexp_b.astype(F32) * knew.astype(BF16).astype(F32), axis=-1, keepdims=True) * scale
        mx = jnp.maximum(jnp.maximum(jnp.max(logits, axis=-1, keepdims=True), lnew), sink)
        p = jnp.exp(logits - mx)
        pn = jnp.exp(lnew - mx)
        den = jnp.sum(p, axis=-1, keepdims=True) + pn + jnp.exp(sink - mx)
        p = p / den
        pn = pn / den
        ob = _dot(p.astype(BF16), cv.astype(BF16)) + pn.astype(BF16).astype(F32) * vnew.astype(BF16).astype(F32)
        for g in range(Q_PER_KV):
            blk = jnp.where(diag, ob[g * SUBLANES:(g + 1) * SUBLANES], 0.0)
            o_ref[b:b + 1, g * D_KV:(g + 1) * D_KV] = jnp.sum(blk, axis=0, keepdims=True).astype(BF16)


def _swa_step(proj, ck, cv, gq_t, gk_t, sink_col, seg01, bb):
    nb = DEC_BATCH // bb
    return pl.pallas_call(
        functools.partial(_swa_step_kernel, bb=bb),
        grid=(nb,),
        in_specs=[
            pl.BlockSpec((bb, D_ATTN), lambda i: (i, OFF_Q // D_ATTN)),
            pl.BlockSpec((bb, D_KV), lambda i: (i, OFF_K // D_KV)),
            pl.BlockSpec((bb, D_KV), lambda i: (i, OFF_V // D_KV)),
            pl.BlockSpec((bb, WINDOW, D_KV), lambda i: (i, 0, 0)),
            pl.BlockSpec((bb, WINDOW, D_KV), lambda i: (i, 0, 0)),
            pl.BlockSpec((1, D_KV), lambda i: (0, 0)),
            pl.BlockSpec((1, D_KV), lambda i: (0, 0)),
            pl.BlockSpec((Q_PER_KV * SUBLANES, 1), lambda i: (0, 0)),
            pl.BlockSpec((D_KV, D_KV), lambda i: (0, 0)),
        ],
        out_specs=[
            pl.BlockSpec((bb, D_ATTN), lambda i: (i, 0)),
            pl.BlockSpec((bb, WINDOW, D_KV), lambda i: (i, 0, 0)),
            pl.BlockSpec((bb, WINDOW, D_KV), lambda i: (i, 0, 0)),
        ],
        out_shape=[
            jax.ShapeDtypeStruct((DEC_BATCH, D_ATTN), BF16),
            jax.ShapeDtypeStruct((DEC_BATCH, WINDOW, D_KV), F32),
            jax.ShapeDtypeStruct((DEC_BATCH, WINDOW, D_KV), F32),
        ],
        compiler_params=_cparams(("parallel",)),
        name="swa_step",
    )(proj, proj, proj, ck, cv, gq_t, gk_t, sink_col, seg01)


def _ssd_step_kernel(z_ref, xs_ref, bc_ref, dt_ref, px_ref, pb_ref, h_ref, cwx_ref, cbx_ref, cwb_ref, cbb_ref,
                     dtb_ref, alog_ref, dskip_ref, ng_ref, e_ref, gx_ref, y_ref, nx_ref, nb_ref, ho_ref, *, bb):
    xs_raw = xs_ref[...]
    bc_raw = bc_ref[...]
    cwx = cwx_ref[...]
    cwb = cwb_ref[...]
    xs = _silu(px_ref[0] * cwx[0:1] + px_ref[1] * cwx[1:2] + px_ref[2] * cwx[2:3] + xs_raw * cwx[3:4] + cbx_ref[...])
    bc = _silu(pb_ref[0] * cwb[0:1] + pb_ref[1] * cwb[1:2] + pb_ref[2] * cwb[2:3] + bc_raw * cwb[3:4] + cbb_ref[...])
    nx_ref[0] = px_ref[1]
    nx_ref[1] = px_ref[2]
    nx_ref[2] = xs_raw
    nb_ref[0] = pb_ref[1]
    nb_ref[1] = pb_ref[2]
    nb_ref[2] = bc_raw

    e01 = e_ref[...]
    dtv = _softplus(dt_ref[...] + dtb_ref[...])
    da = jnp.exp(dtv * (-jnp.exp(alog_ref[...])))
    dt_x = _dot_x01(dtv, e01)
    da_x = _dot_x01(da, e01)
    xdt = xs * dt_x
    bm = bc[:, :D_BC]
    cm = bc[:, D_BC:]
    cb_x = _dot_x01(cm * bm, gx_ref[...])

    lane = lax.broadcasted_iota(jnp.int32, (SUBLANES, D_SSD), 1)
    grp = lax.broadcasted_iota(jnp.int32, (SUBLANES, D_SSD), 0)
    in_group = lane // (HEADS_PER_GROUP * SSD_P) == grp
    pad_n = jnp.zeros((SUBLANES - SSD_GROUPS, D_STATE), F32)
    zeros_n = jnp.zeros((SUBLANES, D_STATE), F32)
    ones3 = jnp.where(lax.broadcasted_iota(jnp.int32, (SUBLANES, D_STATE), 0) < 3, 1.0, 0.0)
    rhs_splat = jnp.concatenate([zeros_n, zeros_n, zeros_n, ones3], axis=0)
    pad_x = jnp.zeros((SUBLANES - 3, D_SSD), F32)
    y_rows = []
    for b in range(bb):
        h0 = h_ref[b].reshape(D_SSD, D_STATE)
        crow = jnp.concatenate(
            [cm[b:b + 1, g * D_STATE:(g + 1) * D_STATE] for g in range(SSD_GROUPS)] + [pad_n], axis=0)
        yc = _dot_nt(crow.astype(BF16), h0.astype(BF16))
        y_rows.append(jnp.sum(jnp.where(in_group, yc, 0.0), axis=0, keepdims=True))

        x_hi, x_lo = _hi_lo(xdt[b:b + 1])
        d_hi, d_r = _hi_lo(da_x[b:b + 1])
        d_mid, d_lo = _hi_lo(d_r)
        brow = jnp.concatenate(
            [bm[b:b + 1, g * D_STATE:(g + 1) * D_STATE] for g in range(SSD_GROUPS)] + [pad_n], axis=0)
        b_hi, b_lo = _hi_lo(brow)
        xm_hi = jnp.where(in_group, x_hi, 0.0)
        xm_lo = jnp.where(in_group, x_lo, 0.0)
        lhs = jnp.concatenate([xm_hi, xm_lo, xm_hi, d_hi, d_mid, d_lo, pad_x], axis=0).astype(BF16)
        rhs = jnp.concatenate(
            [jnp.concatenate([b_hi, b_hi, b_lo, zeros_n], axis=0), rhs_splat], axis=1).astype(BF16)
        both = _dot_tn(lhs, rhs)
        hn = h0 * both[:, D_STATE:] + both[:, :D_STATE]
        ho_ref[b] = hn.reshape(SSD_HEADS, SSD_P, D_STATE)
    yc_all = jnp.concatenate(y_rows, axis=0)
    y = da_x * yc_all + cb_x * xdt + xs * dskip_ref[...]
    y = y * _silu(z_ref[...])
    y_ref[...] = _rms(y, ng_ref[...]).astype(BF16)


def _ssd_step(proj, px, pb, h0, cwx, cbx, cwb, cbb, dtb, alog, dskip_x, ng, e01, gx01, bb):
    nb = DEC_BATCH // bb
    const = lambda i: (0, 0)
    return pl.pallas_call(
        functools.partial(_ssd_step_kernel, bb=bb),
        grid=(nb,),
        in_specs=[
            pl.BlockSpec((bb, D_SSD), lambda i: (i, OFF_Z // D_SSD)),
            pl.BlockSpec((bb, D_SSD), lambda i: (i, OFF_XS // D_SSD)),
            pl.BlockSpec((bb, 2 * D_BC), lambda i: (i, OFF_BC // (2 * D_BC))),
            pl.BlockSpec((bb, LANES), lambda i: (i, OFF_DT // LANES)),
            pl.BlockSpec((SSD_CONV - 1, bb, D_SSD), lambda i: (0, i, 0)),
            pl.BlockSpec((SSD_CONV - 1, bb, 2 * D_BC), lambda i: (0, i, 0)),
            pl.BlockSpec((bb, SSD_HEADS, SSD_P, D_STATE), lambda i: (i, 0, 0, 0)),
            pl.BlockSpec((SSD_CONV, D_SSD), const),
            pl.BlockSpec((1, D_SSD), const),
            pl.BlockSpec((SSD_CONV, 2 * D_BC), const),
            pl.BlockSpec((1, 2 * D_BC), const),
            pl.BlockSpec((1, LANES), const),
            pl.BlockSpec((1, LANES), const),
            pl.BlockSpec((1, D_SSD), const),
            pl.BlockSpec((1, D_SSD), const),
            pl.BlockSpec((LANES, D_SSD), const),
            pl.BlockSpec((D_BC, D_SSD), const),
        ],
        out_specs=[
            pl.BlockSpec((bb, D_SSD), lambda i: (i, 0)),
            pl.BlockSpec((SSD_CONV - 1, bb, D_SSD), lambda i: (0, i, 0)),
            pl.BlockSpec((SSD_CONV - 1, bb, 2 * D_BC), lambda i: (0, i, 0)),
            pl.BlockSpec((bb, SSD_HEADS, SSD_P, D_STATE), lambda i: (i, 0, 0, 0)),
        ],
        out_shape=[
            jax.ShapeDtypeStruct((DEC_BATCH, D_SSD), BF16),
            jax.ShapeDtypeStruct((SSD_CONV - 1, DEC_BATCH, D_SSD), F32),
            jax.ShapeDtypeStruct((SSD_CONV - 1, DEC_BATCH, 2 * D_BC), F32),
            jax.ShapeDtypeStruct((DEC_BATCH, SSD_HEADS, SSD_P, D_STATE), F32),
        ],
        compiler_params=_cparams(("parallel",)),
        name="ssd_step",
    )(proj, proj, proj, proj, px, pb, h0, cwx, cbx, cwb, cbb, dtb, alog, dskip_x, ng, e01, gx01)


def _cross_step_kernel(x_ref, g_ref, wq_ref, gq_ref, mk_ref, mv_ref, wo_ref, o_ref, oat_ref, *, bb):
    x = x_ref[...]
    q = _dot(_rms(x, g_ref[...]).astype(BF16), wq_ref[...])
    gq = gq_ref[...]
    qn = jnp.concatenate([_rms(q[:, h * MEM_HD:(h + 1) * MEM_HD], gq) for h in range(MEM_HEADS)], axis=1)
    rr = lax.broadcasted_iota(jnp.int32, (SUBLANES, D_XATTN), 0)
    ll = lax.broadcasted_iota(jnp.int32, (SUBLANES, D_XATTN), 1)
    diag = rr == ll // MEM_HD
    scale = MEM_HD ** -0.5
    for b in range(bb):
        qexp = jnp.where(diag, jnp.broadcast_to(qn[b:b + 1], (SUBLANES, D_XATTN)), 0.0).astype(BF16)
        logits = _dot_nt(qexp, mk_ref[b].astype(BF16)) * scale
        mx = jnp.max(logits, axis=-1, keepdims=True)
        p = jnp.exp(logits - mx)
        p = p / jnp.sum(p, axis=-1, keepdims=True)
        ob = _dot(p.astype(BF16), mv_ref[b].astype(BF16))
        oat_ref[b:b + 1, :] = jnp.sum(jnp.where(diag, ob, 0.0), axis=0, keepdims=True)
    o_ref[...] = x + _dot(oat_ref[...].astype(BF16), wo_ref[...])


def _cross_step(x, g, wq, gq, mk, mv, wo, bb):
    nb = DEC_BATCH // bb
    return pl.pallas_call(
        functools.partial(_cross_step_kernel, bb=bb),
        grid=(nb,),
        in_specs=[
            pl.BlockSpec((bb, D_MODEL), lambda i: (i, 0)),
            pl.BlockSpec((1, D_MODEL), lambda i: (0, 0)),
            pl.BlockSpec((D_MODEL, D_XATTN), lambda i: (0, 0)),
            pl.BlockSpec((1, MEM_HD), lambda i: (0, 0)),
            pl.BlockSpec((bb, N_MEM, D_XATTN), lambda i: (i, 0, 0)),
            pl.BlockSpec((bb, N_MEM, D_XATTN), lambda i: (i, 0, 0)),
            pl.BlockSpec((D_XATTN, D_MODEL), lambda i: (0, 0)),
        ],
        out_specs=pl.BlockSpec((bb, D_MODEL), lambda i: (i, 0)),
        out_shape=jax.ShapeDtypeStruct((DEC_BATCH, D_MODEL), F32),
        scratch_shapes=[pltpu.VMEM((bb, D_XATTN), F32)],
        compiler_params=_cparams(("parallel",)),
        name="cross_step",
    )(x, g, wq, gq, mk, mv, wo)


def _ffn_step_kernel(x_ref, g_ref, wg_ref, wv_ref, cwg_ref, cbg_ref, cwv_ref, cbv_ref, pg_ref, pv_ref, wd_ref,
                     o_ref, ng_ref, nv_ref, xn_ref):
    @pl.when(pl.program_id(0) == 0)
    def _():
        xn_ref[...] = _rms(x_ref[...], g_ref[...]).astype(BF16)
        o_ref[...] = x_ref[...]

    xn = xn_ref[...]

    def conv(u, p_ref, cw, cb):
        return p_ref[0] * cw[0:1] + p_ref[1] * cw[1:2] + u * cw[2:3] + cb

    ug = _dot(xn, wg_ref[...])
    uv = _dot(xn, wv_ref[...])
    ng_ref[0] = pg_ref[1]
    ng_ref[1] = ug
    nv_ref[0] = pv_ref[1]
    nv_ref[1] = uv
    hcv = _silu(conv(ug, pg_ref, cwg_ref[...], cbg_ref[...])) * conv(uv, pv_ref, cwv_ref[...], cbv_ref[...])
    o_ref[...] += _dot(hcv.astype(BF16), wd_ref[...])


def _ffn_step(x, g, wup, cw, cb, past, wd, tc):
    m = x.shape[0]
    ncol = D_FF // tc
    return pl.pallas_call(
        _ffn_step_kernel,
        grid=(ncol,),
        in_specs=[
            pl.BlockSpec((m, D_MODEL), lambda c: (0, 0)),
            pl.BlockSpec((1, D_MODEL), lambda c: (0, 0)),
            pl.BlockSpec((D_MODEL, tc), lambda c: (0, c)),
            pl.BlockSpec((D_MODEL, tc), lambda c: (0, ncol + c)),
            pl.BlockSpec((FFN_CONV, tc), lambda c: (0, c)),
            pl.BlockSpec((1, tc), lambda c: (0, c)),
            pl.BlockSpec((FFN_CONV, tc), lambda c: (0, ncol + c)),
            pl.BlockSpec((1, tc), lambda c: (0, ncol + c)),
            pl.BlockSpec((FFN_CONV - 1, m, tc), lambda c: (0, 0, c)),
            pl.BlockSpec((FFN_CONV - 1, m, tc), lambda c: (0, 0, ncol + c)),
            pl.BlockSpec((tc, D_MODEL), lambda c: (c, 0)),
        ],
        out_specs=[
            pl.BlockSpec((m, D_MODEL), lambda c: (0, 0)),
            pl.BlockSpec((FFN_CONV - 1, m, tc), lambda c: (0, 0, c)),
            pl.BlockSpec((FFN_CONV - 1, m, tc), lambda c: (0, 0, c)),
        ],
        out_shape=[
            jax.ShapeDtypeStruct((m, D_MODEL), F32),
            jax.ShapeDtypeStruct((FFN_CONV - 1, m, D_FF), F32),
            jax.ShapeDtypeStruct((FFN_CONV - 1, m, D_FF), F32),
        ],
        scratch_shapes=[pltpu.VMEM((m, D_MODEL), BF16)],
        compiler_params=_cparams(("arbitrary",)),
        name="ffn_step",
    )(x, g, wup, wup, cw, cb, cw, cb, past, past, wd)


def _head_expand01():
    e = np.zeros((LANES, D_SSD), np.float32)
    for h in range(SSD_HEADS):
        e[h, h * SSD_P:(h + 1) * SSD_P] = 1.0
    return jnp.asarray(e, BF16)


def _segment01():
    i = np.arange(D_KV)
    return jnp.asarray((i[:, None] // HEAD_DIM == i[None, :] // HEAD_DIM).astype(np.float32), BF16)


def _group_expand01():
    r = np.arange(D_BC)[:, None] // D_STATE
    c = np.arange(D_SSD)[None, :] // (HEADS_PER_GROUP * SSD_P)
    return jnp.asarray((r == c).astype(np.float32), BF16)


def kernel(x_prompt, x_sample, cache_swa_k, cache_swa_v, state_ssd_conv, state_ssd, cache_mem_k, cache_mem_v,
           state_ffn_conv, mem_prompt, g_mix, w_in, q_norm_g, k_norm_g, sinks, ssd_conv_w, ssd_conv_b, dt_bias,
           a_log, d_skip, ssd_norm_g, w_out, g_cross, g_mem, w_cq, w_ck, w_cv, cq_norm_g, ck_norm_g, w_co, g_ffn,
           w_up, ffn_conv_w, ffn_conv_b, w_down):
    w = w_in[0]
    wq = w[:, :D_ATTN].reshape(D_MODEL, N_KV, Q_PER_KV, HEAD_DIM).transpose(0, 2, 1, 3).reshape(D_MODEL, D_ATTN)
    c_k, c_v, c_z, c_x = D_ATTN, D_ATTN + D_KV, D_ATTN + 2 * D_KV, D_ATTN + 2 * D_KV + D_SSD
    c_dt = c_x + CONV_DIM
    w_proj = jnp.concatenate([
        w[:, c_z:c_x], w[:, c_x:c_x + D_SSD], wq, w[:, c_x + D_SSD:c_dt], w[:, c_k:c_v], w[:, c_v:c_z],
        w[:, c_dt:], jnp.zeros((D_MODEL, N_PROJ - OFF_DT - SSD_HEADS), F32)], axis=1).astype(BF16)
    wo = w_out[0]
    wo_a = wo[:D_ATTN].reshape(N_KV, Q_PER_KV, HEAD_DIM, D_MODEL).transpose(1, 0, 2, 3).reshape(D_ATTN, D_MODEL)
    wo_a = wo_a.astype(BF16)
    wo_s = wo[D_ATTN:].astype(BF16)
    w_cq_b = w_cq[0].astype(BF16)
    w_ckv_b = jnp.concatenate([w_ck[0], w_cv[0]], axis=1).astype(BF16)
    w_co_b = w_co[0].astype(BF16)
    w_up_b = w_up[0].astype(BF16)
    w_down_b = w_down[0].astype(BF16)

    g_mix2 = g_mix[0][None]
    g_cross2 = g_cross[0][None]
    g_mem2 = g_mem[0][None]
    g_ffn2 = g_ffn[0][None]
    gq = q_norm_g[0][None]
    gk = k_norm_g[0][None]
    gq_t = jnp.tile(gq, (1, N_KV))
    gk_t = jnp.tile(gk, (1, N_KV))
    cq_g = cq_norm_g[0][None]
    ck_g = ck_norm_g[0][None]
    cw = ssd_conv_w[0]
    cwx, cwb = cw[:, :D_SSD], cw[:, D_SSD:]
    cbx, cbb = ssd_conv_b[0][None, :D_SSD], ssd_conv_b[0][None, D_SSD:]
    pad_h = LANES - SSD_HEADS
    dtb = jnp.pad(dt_bias[0], (0, pad_h))[None]
    alog = jnp.pad(a_log[0], (0, pad_h))[None]
    dskip_x = jnp.repeat(d_skip[0], SSD_P)[None]
    ng = ssd_norm_g[0][None]
    fcw = ffn_conv_w[0]
    fcb = ffn_conv_b[0][None]
    e01 = _head_expand01()
    seg01 = _segment01()
    gx01 = _group_expand01()
    sink_gk = sinks[0].reshape(N_KV, Q_PER_KV).T
    sink_col = jnp.pad(sink_gk, ((0, 0), (0, SUBLANES - N_KV))).reshape(Q_PER_KV * SUBLANES, 1)

    xp = x_prompt.reshape(BATCH * SEQ, D_MODEL)
    proj_p = _norm_matmul(xp, g_mix2, w_proj, 1024, TN_PROJ)
    attn_p, kn_last = _swa_prompt(proj_p, sinks[0], gq, gk)
    ssd_p, st_p = _ssd_prompt(proj_p, cwx, cbx, cwb, cbb, dtb, alog, dskip_x, ng, e01)
    x1_p = _out_proj(attn_p, ssd_p, wo_a, wo_s, xp, 512, 1024)
    mk_p, mv_p = _mem_kv(mem_prompt.reshape(BATCH * N_MEM, D_MODEL), g_mem2, w_ckv_b, ck_g)
    x2_p = _cross_prompt(x1_p, g_cross2, w_cq_b, cq_g, mk_p, mv_p, w_co_b, 512)
    tm_ffn = 512
    y_p, tail_g, tail_v = _ffn_prompt(x2_p, g_ffn2, w_up_b, fcw, fcb, w_down_b, tm_ffn, 512)

    proj_p3 = proj_p.reshape(BATCH, SEQ, N_PROJ)
    swa_k_prompt = kn_last.reshape(1, BATCH, WINDOW, N_KV, HEAD_DIM)
    swa_v_prompt = proj_p3[:, SEQ - WINDOW:, OFF_V:OFF_V + D_KV].reshape(1, BATCH, WINDOW, N_KV, HEAD_DIM)
    ssd_conv_prompt = jnp.concatenate(
        [proj_p3[:, SEQ - (SSD_CONV - 1):, OFF_XS:OFF_XS + D_SSD],
         proj_p3[:, SEQ - (SSD_CONV - 1):, OFF_BC:OFF_BC + 2 * D_BC]], axis=-1)[None]
    ssd_state_prompt = st_p.reshape(BATCH, D_STATE, SSD_HEADS, SSD_P).transpose(0, 2, 3, 1)[None]
    mem_k_prompt = mk_p.reshape(1, BATCH, N_MEM, MEM_HEADS, MEM_HD)
    mem_v_prompt = mv_p.reshape(1, BATCH, N_MEM, MEM_HEADS, MEM_HD)
    last_tiles = (jnp.arange(BATCH) + 1) * (SEQ // tm_ffn) - 1
    ffn_conv_prompt = jnp.concatenate(
        [tail_g[last_tiles, SUBLANES - (FFN_CONV - 1):], tail_v[last_tiles, SUBLANES - (FFN_CONV - 1):]], axis=-1)[None]

    xs_ = x_sample.reshape(DEC_BATCH, D_MODEL)
    proj_s = _norm_matmul(xs_, g_mix2, w_proj, DEC_BATCH, TN_PROJ)
    attn_s, nk_s, nv_s = _swa_step(
        proj_s, cache_swa_k[0].reshape(DEC_BATCH, WINDOW, D_KV), cache_swa_v[0].reshape(DEC_BATCH, WINDOW, D_KV),
        gq_t, gk_t, sink_col, seg01, 8)
    past = state_ssd_conv[0].transpose(1, 0, 2)
    ssd_s, ncx, ncb, h_s = _ssd_step(
        proj_s, past[:, :, :D_SSD], past[:, :, D_SSD:], state_ssd[0], cwx, cbx, cwb, cbb, dtb, alog, dskip_x, ng,
        e01, gx01, 8)
    x1_s = _out_proj(attn_s, ssd_s, wo_a, wo_s, xs_, DEC_BATCH, 1024)
    x2_s = _cross_step(
        x1_s, g_cross2, w_cq_b, cq_g, cache_mem_k[0].reshape(DEC_BATCH, N_MEM, D_XATTN),
        cache_mem_v[0].reshape(DEC_BATCH, N_MEM, D_XATTN), w_co_b, 8)
    y_s, nfg, nfv = _ffn_step(x2_s, g_ffn2, w_up_b, fcw, fcb, state_ffn_conv[0].transpose(1, 0, 2), w_down_b, 512)

    swa_k_sample = nk_s.reshape(1, DEC_BATCH, WINDOW, N_KV, HEAD_DIM)
    swa_v_sample = nv_s.reshape(1, DEC_BATCH, WINDOW, N_KV, HEAD_DIM)
    ssd_conv_sample = jnp.concatenate([ncx, ncb], axis=-1).transpose(1, 0, 2)[None]
    ffn_conv_sample = jnp.concatenate([nfg, nfv], axis=-1).transpose(1, 0, 2)[None]

    return (y_p.reshape(BATCH, SEQ, D_MODEL), y_s.reshape(DEC_BATCH, 1, D_MODEL), swa_k_prompt, swa_v_prompt,
            swa_k_sample, swa_v_sample, ssd_conv_prompt, ssd_conv_sample, ssd_state_prompt, h_s[None],
            mem_k_prompt, mem_v_prompt, ffn_conv_prompt, ffn_conv_sample)
```
